```python
import math
import jax, jax.numpy as jnp
from jax import lax
import numpy as np

D_MODEL = 1024
BATCH = 8
SEQ = 4096
DEPTH = 1

POOL_WINDOWS = (2, 4, 8, 16)
POOL_WIDTH = D_MODEL // 2
POOL_GROUP = POOL_WIDTH // len(POOL_WINDOWS)
ATTN_WIDTH = D_MODEL - POOL_WIDTH
HEAD_DIM = 64
N_HEADS = ATTN_WIDTH // HEAD_DIM
IN_WIDTH = POOL_WIDTH + 3 * ATTN_WIDTH
MOBA_BLOCK = 256
MOBA_TOP_K = 3
QUERY_CHUNK = 32
NUM_BUCKETS = 32
MAX_DISTANCE = 128
D_FF = 4 * D_MODEL
EPS = 1e-6

kernel_name = "hymba_pool_moba_sandwich_block"


def rmsnorm(x, g):
    xf = x.astype(jnp.float32)
    y = xf * lax.rsqrt(jnp.mean(xf * xf, axis=-1, keepdims=True) + EPS)
    return (y * g.astype(jnp.float32)).astype(x.dtype)


def t5_bucket(dist):
    n = jnp.maximum(dist, 0)
    max_exact = NUM_BUCKETS // 2
    n_safe = jnp.maximum(n, max_exact).astype(jnp.float32)
    large = max_exact + (jnp.log(n_safe / max_exact) / math.log(MAX_DISTANCE / max_exact)
                         * (NUM_BUCKETS - max_exact)).astype(jnp.int32)
    large = jnp.minimum(large, NUM_BUCKETS - 1)
    return jnp.where(n < max_exact, n, large)


def pool_mixer(u, w_pool, pool_scale):
    b, s, _ = u.shape
    uf = u.astype(jnp.float32).reshape(b, s, len(POOL_WINDOWS), POOL_GROUP)
    csum = jnp.cumsum(uf, axis=1)
    pos = jnp.arange(1, s + 1, dtype=jnp.float32)
    pooled = []
    for gi, w in enumerate(POOL_WINDOWS):
        c = csum[:, :, gi]
        prev = jnp.pad(c, ((0, 0), (w, 0), (0, 0)))[:, :s]
        cnt = jnp.minimum(pos, float(w))[None, :, None]
        pooled.append((c - prev) / cnt)
    mixed = (jnp.stack(pooled, axis=2) - uf).astype(u.dtype)
    y = jnp.einsum('bsgc,gcd->bsgd', mixed, w_pool).reshape(b, s, POOL_WIDTH)
    return y * pool_scale


def moba_attention(q, k, v, rel_bias):
    b, s, h, dh = q.shape
    n_blocks = -(-s // MOBA_BLOCK)
    s_pad = n_blocks * MOBA_BLOCK
    pad = ((0, 0), (0, s_pad - s), (0, 0), (0, 0))
    q, k, v = [jnp.pad(t, pad).transpose(0, 2, 1, 3) for t in (q, k, v)]
    scale = HEAD_DIM ** -0.5
    k_blocks = k.reshape(b, h, n_blocks, MOBA_BLOCK, dh)
    v_blocks = v.reshape(b, h, n_blocks, MOBA_BLOCK, dh)

    k_mean = jnp.mean(k_blocks.astype(jnp.float32), axis=3)
    gate = jnp.einsum('bhsd,bhnd->bhsn', q.astype(jnp.float32), k_mean)
    q_block = jnp.arange(s_pad) // MOBA_BLOCK
    past = jnp.arange(n_blocks)[None, :] < q_block[:, None]
    gate = jnp.where(past, gate, -jnp.inf)
    top_k = min(MOBA_TOP_K, n_blocks)
    _, sel = lax.top_k(gate, top_k)
    sel_valid = sel < q_block[:, None]

    n_chunks = s_pad // QUERY_CHUNK

    def to_chunks(t):
        t = t.reshape((b, h, n_chunks, QUERY_CHUNK) + t.shape[3:])
        return jnp.moveaxis(t, 2, 0)

    bias_table = rel_bias.T.astype(jnp.float32)
    bi = jnp.arange(b)[:, None, None, None]
    hi = jnp.arange(h)[None, :, None, None]
    offs = jnp.arange(MOBA_BLOCK)

    def chunk_attn(args):
        q_c, sel_c, valid_c, ci = args
        q_pos = ci * QUERY_CHUNK + jnp.arange(QUERY_CHUNK)
        k_sel = k_blocks[bi, hi, sel_c]
        v_sel = v_blocks[bi, hi, sel_c]
        s_sel = jnp.einsum('bhcd,bhcknd->bhckn', q_c, k_sel).astype(jnp.float32) * scale
        k_pos_sel = sel_c[..., None] * MOBA_BLOCK + offs
        s_sel = s_sel + bias_table[hi[..., None], t5_bucket(q_pos[:, None, None] - k_pos_sel)]
        s_sel = jnp.where(valid_c[..., None], s_sel, -jnp.inf)
        s_sel = s_sel.reshape(b, h, QUERY_CHUNK, top_k * MOBA_BLOCK)
        own = (ci * QUERY_CHUNK) // MOBA_BLOCK
        k_own = lax.dynamic_index_in_dim(k_blocks, own, axis=2, keepdims=False)
        v_own = lax.dynamic_index_in_dim(v_blocks, own, axis=2, keepdims=False)
        dist_own = q_pos[:, None] - (own * MOBA_BLOCK + offs)[None, :]
        s_own = (jnp.einsum('bhcd,bhnd->bhcn', q_c, k_own).astype(jnp.float32) * scale
                 + bias_table[:, t5_bucket(dist_own)][None])
        s_own = jnp.where(dist_own >= 0, s_own, -jnp.inf)
        p = jax.nn.softmax(jnp.concatenate([s_sel, s_own], axis=-1), axis=-1).astype(v.dtype)
        p_sel = p[..., :top_k * MOBA_BLOCK].reshape(b, h, QUERY_CHUNK, top_k, MOBA_BLOCK)
        p_own = p[..., top_k * MOBA_BLOCK:]
        return (jnp.einsum('bhckn,bhcknd->bhcd', p_sel, v_sel)
                + jnp.einsum('bhcn,bhnd->bhcd', p_own, v_own))

    out = lax.map(chunk_attn, (to_chunks(q), to_chunks(sel), to_chunks(sel_valid),
                               jnp.arange(n_chunks, dtype=jnp.int32)))
    out = jnp.moveaxis(out, 0, 2).reshape(b, h, s_pad, dh)[:, :, :s]
    return out.transpose(0, 2, 1, 3).reshape(b, s, h * dh)


def setup_inputs(seed: int = 0) -> dict:
    key = jax.random.key(seed)
    ks = jax.random.split(key, 16)
    L = DEPTH
    nrm = lambda k, shape, sc: jax.random.normal(k, shape, jnp.float32) * sc
    gain = lambda k, shape: 1.0 + 0.05 * jax.random.normal(k, shape, jnp.float32)
    return {
        "x": jax.random.normal(ks[0], (BATCH, SEQ, D_MODEL), jnp.float32),
        "g_mix_pre": gain(ks[1], (L, D_MODEL)),
        "w_in": nrm(ks[2], (L, D_MODEL, IN_WIDTH), D_MODEL ** -0.5),
        "w_pool": nrm(ks[3], (L, len(POOL_WINDOWS), POOL_GROUP, POOL_GROUP), POOL_GROUP ** -0.5),
        "pool_scale": 1.0 + 0.1 * jax.random.normal(ks[4], (L, POOL_WIDTH), jnp.float32),
        "rel_bias": nrm(ks[5], (NUM_BUCKETS, N_HEADS), 0.5),
        "g_pool_out": gain(ks[6], (L, POOL_WIDTH)),
        "g_attn_out": gain(ks[7], (L, ATTN_WIDTH)),
        "w_out": nrm(ks[8], (L, D_MODEL, D_MODEL), D_MODEL ** -0.5),
        "g_mix_post": gain(ks[9], (L, D_MODEL)),
        "g_mlp_pre": gain(ks[10], (L, D_MODEL)),
        "w_up": nrm(ks[11], (L, D_MODEL, D_FF), D_MODEL ** -0.5),
        "w_down": nrm(ks[12], (L, D_FF, D_MODEL), D_FF ** -0.5),
        "g_mlp_post": gain(ks[13], (L, D_MODEL)),
    }


def reference(x, g_mix_pre, w_in, w_pool, pool_scale, rel_bias, g_pool_out, g_attn_out,
              w_out, g_mix_post, g_mlp_pre, w_up, w_down, g_mlp_post):
    b, s, _ = x.shape
    for i in range(DEPTH):
        h = rmsnorm(x, g_mix_pre[i])
        z = h @ w_in[i]
        u, q, k, v = jnp.split(z, [POOL_WIDTH, POOL_WIDTH + ATTN_WIDTH,
                                   POOL_WIDTH + 2 * ATTN_WIDTH], axis=-1)
        y_pool = rmsnorm(pool_mixer(u, w_pool[i], pool_scale[i]), g_pool_out[i])
        y_attn = moba_attention(q.reshape(b, s, N_HEADS, HEAD_DIM),
                                k.reshape(b, s, N_HEADS, HEAD_DIM),
                                v.reshape(b, s, N_HEADS, HEAD_DIM), rel_bias)
        y_attn = rmsnorm(y_attn, g_attn_out[i])
        mix = jnp.concatenate([y_pool, y_attn], axis=-1) @ w_out[i]
        x = x + rmsnorm(mix, g_mix_post[i])
        h = rmsnorm(x, g_mlp_pre[i])
        ff = jnp.square(jax.nn.relu(h @ w_up[i])) @ w_down[i]
        x = x + rmsnorm(ff, g_mlp_post[i])
    return x
```

```python
import functools
import math

import numpy as np
import jax
import jax.numpy as jnp
from jax import lax
from jax.experimental import pallas as pl
from jax.experimental.pallas import tpu as pltpu

D_MODEL = 1024
POOL_WINDOWS = (2, 4, 8, 16)
POOL_WIDTH = 512
POOL_GROUP = 128
ATTN_WIDTH = 512
HEAD_DIM = 64
N_HEADS = 8
MOBA_BLOCK = 256
MOBA_TOP_K = 3
NUM_BUCKETS = 32
MAX_DISTANCE = 128
D_FF = 4096
EPS = 1e-6

MAX_WINDOW = max(POOL_WINDOWS)
HEADS_PER_STEP = 2
SEQ_TILE = 512
ROW_TILE = 512
FF_CHUNK = 1024
VMEM_LIMIT_BYTES = 56 * 1024 * 1024

_F32 = jnp.float32
_BF16 = jnp.bfloat16
_NEG_INF = float("-inf")


def _bucket_upper_bounds():
    max_exact = NUM_BUCKETS // 2
    n = np.arange(0, 4 * MAX_DISTANCE)
    n_safe = np.maximum(n, max_exact).astype(np.float32)
    val = (np.log(n_safe / np.float32(max_exact)) / np.float32(math.log(MAX_DISTANCE / max_exact))
           * np.float32(NUM_BUCKETS - max_exact))
    large = np.minimum(max_exact + val.astype(np.int32), NUM_BUCKETS - 1)
    bucket = np.where(n < max_exact, n, large)
    return tuple(int(np.max(n[bucket <= b])) for b in range(NUM_BUCKETS - 1))


_BUCKET_UB = _bucket_upper_bounds()


def _rms(x, gain):
    ms = jnp.mean(x * x, axis=-1, keepdims=True)
    return x * lax.rsqrt(ms + EPS) * gain


def _bias_kernel(rb_ref, o_ref):
    h = pl.program_id(0)
    r = lax.broadcasted_iota(jnp.int32, (MOBA_BLOCK, MOBA_BLOCK), 0)
    c = lax.broadcasted_iota(jnp.int32, (MOBA_BLOCK, MOBA_BLOCK), 1)
    d_own = c - r

    def lookup(d):
        val = jnp.full(d.shape, rb_ref[NUM_BUCKETS - 1, h], _F32)
        for b in range(NUM_BUCKETS - 2, -1, -1):
            val = jnp.where(d <= _BUCKET_UB[b], rb_ref[b, h], val)
        return val

    o_ref[0, 0] = jnp.where(d_own >= 0, lookup(d_own), _NEG_INF)
    o_ref[0, 1] = lookup(d_own + MOBA_BLOCK)


def _bias_tiles(rel_bias):
    return pl.pallas_call(
        _bias_kernel,
        grid=(N_HEADS,),
        in_specs=[pl.BlockSpec(memory_space=pltpu.SMEM)],
        out_specs=pl.BlockSpec((1, 2, MOBA_BLOCK, MOBA_BLOCK), lambda h: (h, 0, 0, 0)),
        out_shape=jax.ShapeDtypeStruct((N_HEADS, 2, MOBA_BLOCK, MOBA_BLOCK), _F32),
        name="bias_tiles",
    )(rel_bias)


def _inproj_kernel(x_ref, g_ref, wuk_ref, wqv_ref, wpool_ref, pscale_ref, gpool_ref,
                   ypool_ref, k_ref, qT_ref, vT_ref, sel_ref, ext_ref, gate_w_ref):
    t = pl.program_id(1)
    ts = SEQ_TILE
    blocks_per_tile = ts // MOBA_BLOCK
    n_blocks = gate_w_ref.shape[0] // N_HEADS

    @pl.when(t == 0)
    def _():
        ext_ref[0:MAX_WINDOW, :] = jnp.zeros((MAX_WINDOW, POOL_WIDTH), _F32)
        gate_w_ref[...] = jnp.zeros(gate_w_ref.shape, _F32)

    h = _rms(x_ref[0], g_ref[...]).astype(_BF16)
    uk = jnp.dot(h, wuk_ref[...], preferred_element_type=_F32)
    u = uk[:, :POOL_WIDTH]
    k = uk[:, POOL_WIDTH:]
    qv = lax.dot_general(wqv_ref[...], h, (((1,), (1,)), ((), ())),
                         preferred_element_type=_F32)
    qT = qv[:ATTN_WIDTH]
    vT = qv[ATTN_WIDTH:]

    k_ref[0] = k.astype(_BF16)
    scale = HEAD_DIM ** -0.5
    for i in range(blocks_per_tile):
        cols = slice(i * MOBA_BLOCK, (i + 1) * MOBA_BLOCK)
        qT_ref[0, i] = (qT[:, cols] * scale).astype(_BF16)
        vT_ref[0, i] = vT[:, cols].astype(_BF16)

    ext_ref[MAX_WINDOW:MAX_WINDOW + ts, :] = u
    pos = lax.broadcasted_iota(jnp.int32, (ts, POOL_GROUP), 0) + (t * ts + 1)
    ys = []
    for gi, w in enumerate(POOL_WINDOWS):
        lanes = slice(gi * POOL_GROUP, (gi + 1) * POOL_GROUP)
        win = ext_ref[MAX_WINDOW:MAX_WINDOW + ts, lanes]
        for s in range(1, w):
            win = win + ext_ref[MAX_WINDOW - s:MAX_WINDOW - s + ts, lanes]
        cnt = jnp.minimum(pos, w).astype(_F32)
        mixed = win / cnt - u[:, lanes]
        ys.append(jnp.dot(mixed.astype(_BF16), wpool_ref[gi], preferred_element_type=_F32))
    ext_ref[0:MAX_WINDOW, :] = u[ts - MAX_WINDOW:, :]
    y = jnp.concatenate(ys, axis=-1) * pscale_ref[...]
    ypool_ref[0] = _rms(y, gpool_ref[...]).astype(_BF16)

    lane_head = lax.broadcasted_iota(jnp.int32, (N_HEADS, ATTN_WIDTH), 1) // HEAD_DIM
    head_mask = lane_head == lax.broadcasted_iota(jnp.int32, (N_HEADS, ATTN_WIDTH), 0)
    for i in range(blocks_per_tile):
        km = jnp.mean(k[i * MOBA_BLOCK:(i + 1) * MOBA_BLOCK], axis=0, keepdims=True)
        row0 = pl.multiple_of((t * blocks_per_tile + i) * N_HEADS, N_HEADS)
        gate_w_ref[pl.ds(row0, N_HEADS), :] = jnp.where(head_mask, km, 0.0)
    gate = jnp.dot(gate_w_ref[...], qT, precision=lax.Precision.HIGHEST,
                   preferred_element_type=_F32)

    lane = lax.broadcasted_iota(jnp.int32, (N_HEADS, ts), 1)
    q_block = t * blocks_per_tile + lane // MOBA_BLOCK
    gs = [jnp.where(n < q_block, gate[n * N_HEADS:(n + 1) * N_HEADS], _NEG_INF)
          for n in range(n_blocks)]
    bits = jnp.zeros((N_HEADS, ts), jnp.int32)
    for n in range(n_blocks):
        rank = jnp.zeros((N_HEADS, ts), jnp.int32)
        for m in range(n_blocks):
            if m == n:
                continue
            ahead = (gs[m] >= gs[n]) if m < n else (gs[m] > gs[n])
            rank = rank + jnp.where(ahead, 1, 0)
        bits = bits + jnp.where(rank < MOBA_TOP_K, jnp.where(n < q_block, 1 << n, 0), 0)
    sel_ref[0] = bits


def _in_proj(x, g_mix_pre, w_uk, w_qv_t, w_pool, pool_scale, g_pool_out):
    b, s, d = x.shape
    n_blocks = s // MOBA_BLOCK
    n_tiles = s // SEQ_TILE
    bpt = SEQ_TILE // MOBA_BLOCK
    const2 = lambda bi, ti: (0, 0)
    const3 = lambda bi, ti: (0, 0, 0)
    return pl.pallas_call(
        _inproj_kernel,
        grid=(b, n_tiles),
        in_specs=[
            pl.BlockSpec((1, SEQ_TILE, d), lambda bi, ti: (bi, ti, 0)),
            pl.BlockSpec((1, d), const2),
            pl.BlockSpec(w_uk.shape, const2),
            pl.BlockSpec(w_qv_t.shape, const2),
            pl.BlockSpec(w_pool.shape, const3),
            pl.BlockSpec((1, POOL_WIDTH), const2),
            pl.BlockSpec((1, POOL_WIDTH), const2),
        ],
        out_specs=[
            pl.BlockSpec((1, SEQ_TILE, POOL_WIDTH), lambda bi, ti: (bi, ti, 0)),
            pl.BlockSpec((1, SEQ_TILE, ATTN_WIDTH), lambda bi, ti: (bi, ti, 0)),
            pl.BlockSpec((1, bpt, ATTN_WIDTH, MOBA_BLOCK), lambda bi, ti: (bi, ti, 0, 0)),
            pl.BlockSpec((1, bpt, ATTN_WIDTH, MOBA_BLOCK), lambda bi, ti: (bi, ti, 0, 0)),
            pl.BlockSpec((1, N_HEADS, SEQ_TILE), lambda bi, ti: (bi, 0, ti)),
        ],
        out_shape=[
            jax.ShapeDtypeStruct((b, s, POOL_WIDTH), _BF16),
            jax.ShapeDtypeStruct((b, s, ATTN_WIDTH), _BF16),
            jax.ShapeDtypeStruct((b, n_blocks, ATTN_WIDTH, MOBA_BLOCK), _BF16),
            jax.ShapeDtypeStruct((b, n_blocks, ATTN_WIDTH, MOBA_BLOCK), _BF16),
            jax.ShapeDtypeStruct((b, N_HEADS, s), jnp.int32),
        ],
        scratch_shapes=[
            pltpu.VMEM((MAX_WINDOW + SEQ_TILE, POOL_WIDTH), _F32),
            pltpu.VMEM((n_blocks * N_HEADS, ATTN_WIDTH), _F32),
        ],
        compiler_params=pltpu.CompilerParams(
            dimension_semantics=("arbitrary", "arbitrary"),
            vmem_limit_bytes=VMEM_LIMIT_BYTES),
        name="in_proj",
    )(x, g_mix_pre, w_uk, w_qv_t, w_pool, pool_scale, g_pool_out)


def _attn_kernel(rb_ref, qT_ref, k_ref, vT_ref, sel_ref, bias_ref, o_ref):
    hp = pl.program_id(1)
    n_blocks = qT_ref.shape[1]
    lanes = HEADS_PER_STEP * HEAD_DIM
    row = lax.broadcasted_iota(jnp.int32, (lanes, MOBA_BLOCK), 0)

    def key_block(j):
        return k_ref[0, pl.ds(pl.multiple_of(j * MOBA_BLOCK, MOBA_BLOCK), MOBA_BLOCK), :]

    def q_tile(i, carry):
        q2 = qT_ref[0, i]
        outs = []
        for a in range(HEADS_PER_STEP):
            rows = slice(a * HEAD_DIM, (a + 1) * HEAD_DIM)
            qa = jnp.where(row // HEAD_DIM == a, q2, jnp.zeros_like(q2))
            far_bias = rb_ref[NUM_BUCKETS - 1, hp * HEADS_PER_STEP + a]
            sel_row = sel_ref[0, a, pl.ds(i, 1), :]

            def block_mask(j, sel_row=sel_row):
                return (lax.shift_right_logical(sel_row, j) & 1) == 1

            def update(s, v_blk, m, l, acc):
                m_new = jnp.maximum(m, jnp.max(s, axis=0, keepdims=True))
                alpha = jnp.exp(m - m_new)
                p = jnp.exp(s - m_new)
                l = alpha * l + jnp.sum(p, axis=0, keepdims=True)
                acc = alpha * acc + jnp.dot(v_blk, p.astype(_BF16), preferred_element_type=_F32)
                return m_new, l, acc

            s = jnp.dot(key_block(i), qa, preferred_element_type=_F32) + bias_ref[a, 0]
            m = jnp.max(s, axis=0, keepdims=True)
            p = jnp.exp(s - m)
            l = jnp.sum(p, axis=0, keepdims=True)
            acc = jnp.dot(vT_ref[0, i, rows, :], p.astype(_BF16), preferred_element_type=_F32)

            jp = jnp.maximum(i - 1, 0)
            s = (jnp.dot(key_block(jp), qa, preferred_element_type=_F32) + bias_ref[a, 1]
                 + jnp.where(block_mask(jp), 0.0, _NEG_INF))
            m, l, acc = update(s, vT_ref[0, jp, rows, :], m, l, acc)

            def kv_step(j, mla, qa=qa, rows=rows, far_bias=far_bias, block_mask=block_mask,
                        update=update):
                s = (jnp.dot(key_block(j), qa, preferred_element_type=_F32)
                     + jnp.where(block_mask(j), far_bias, _NEG_INF))
                return update(s, vT_ref[0, j, rows, :], *mla)

            m, l, acc = lax.fori_loop(0, jnp.maximum(i - 1, 0), kv_step, (m, l, acc))
            outs.append(acc / l)
        o = jnp.concatenate(outs, axis=0).T
        o_ref[0, pl.ds(pl.multiple_of(i * MOBA_BLOCK, MOBA_BLOCK), MOBA_BLOCK), :] = o
        return carry

    lax.fori_loop(0, n_blocks, q_tile, 0)


def _attention(rel_bias, qT, k, vT, sel, bias_tiles):
    b, n_blocks, _, _ = qT.shape
    s = n_blocks * MOBA_BLOCK
    lanes = HEADS_PER_STEP * HEAD_DIM
    n_pairs = N_HEADS // HEADS_PER_STEP
    return pl.pallas_call(
        _attn_kernel,
        grid=(b, n_pairs),
        in_specs=[
            pl.BlockSpec(memory_space=pltpu.SMEM),
            pl.BlockSpec((1, n_blocks, lanes, MOBA_BLOCK), lambda bi, hp: (bi, 0, hp, 0)),
            pl.BlockSpec((1, s, lanes), lambda bi, hp: (bi, 0, hp)),
            pl.BlockSpec((1, n_blocks, lanes, MOBA_BLOCK), lambda bi, hp: (bi, 0, hp, 0)),
            pl.BlockSpec((1, HEADS_PER_STEP, n_blocks, MOBA_BLOCK), lambda bi, hp: (bi, hp, 0, 0)),
            pl.BlockSpec((HEADS_PER_STEP, 2, MOBA_BLOCK, MOBA_BLOCK), lambda bi, hp: (hp, 0, 0, 0)),
        ],
        out_specs=pl.BlockSpec((1, s, lanes), lambda bi, hp: (bi, 0, hp)),
        out_shape=jax.ShapeDtypeStruct((b, s, ATTN_WIDTH), _F32),
        compiler_params=pltpu.CompilerParams(
            dimension_semantics=("arbitrary", "arbitrary"),
            vmem_limit_bytes=VMEM_LIMIT_BYTES),
        name="moba_attn",
    )(rel_bias, qT, k, vT, sel, bias_tiles)


def _outmlp_kernel(x_ref, ypool_ref, attn_ref, gattn_ref, wop_ref, woa_ref, gpost_ref,
                   gpre_ref, wup_ref, wdown_ref, gmlp_ref, o_ref):
    ya = _rms(attn_ref[...], gattn_ref[...]).astype(_BF16)
    mix = (jnp.dot(ypool_ref[...], wop_ref[...], preferred_element_type=_F32)
           + jnp.dot(ya, woa_ref[...], preferred_element_type=_F32))
    x1 = x_ref[...] + _rms(mix, gpost_ref[...])
    h2 = _rms(x1, gpre_ref[...]).astype(_BF16)
    ff = jnp.zeros(x1.shape, _F32)
    for c in range(D_FF // FF_CHUNK):
        cols = slice(c * FF_CHUNK, (c + 1) * FF_CHUNK)
        up = jnp.dot(h2, wup_ref[:, cols], preferred_element_type=_F32)
        act = jnp.square(jnp.maximum(up, 0.0)).astype(_BF16)
        ff = ff + jnp.dot(act, wdown_ref[cols, :], preferred_element_type=_F32)
    o_ref[...] = x1 + _rms(ff, gmlp_ref[...])


def _out_mlp(x2, ypool2, attn2, g_attn_out, w_out_pool, w_out_attn, g_mix_post,
             g_mlp_pre, w_up, w_down, g_mlp_post):
    rows, d = x2.shape
    row_map = lambda i: (i, 0)
    const = lambda i: (0, 0)

    def resident(shape):
        return pl.BlockSpec(shape, const, pipeline_mode=pl.Buffered(1))

    return pl.pallas_call(
        _outmlp_kernel,
        grid=(rows // ROW_TILE,),
        in_specs=[
            pl.BlockSpec((ROW_TILE, d), row_map),
            pl.BlockSpec((ROW_TILE, POOL_WIDTH), row_map),
            pl.BlockSpec((ROW_TILE, ATTN_WIDTH), row_map),
            resident((1, ATTN_WIDTH)),
            resident(w_out_pool.shape),
            resident(w_out_attn.shape),
            resident((1, d)),
            resident((1, d)),
            resident(w_up.shape),
            resident(w_down.shape),
            resident((1, d)),
        ],
        out_specs=pl.BlockSpec((ROW_TILE, d), row_map),
        out_shape=jax.ShapeDtypeStruct((rows, d), _F32),
        compiler_params=pltpu.CompilerParams(
            dimension_semantics=("arbitrary",),
            vmem_limit_bytes=VMEM_LIMIT_BYTES),
        name="out_mlp",
    )(x2, ypool2, attn2, g_attn_out, w_out_pool, w_out_attn, g_mix_post,
      g_mlp_pre, w_up, w_down, g_mlp_post)


def kernel(x, g_mix_pre, w_in, w_pool, pool_scale, rel_bias, g_pool_out, g_attn_out, w_out,
           g_mix_post, g_mlp_pre, w_up, w_down, g_mlp_post):
    b, s, d = x.shape
    assert g_mix_pre.shape[0] == 1, "single-layer block"
    assert s % SEQ_TILE == 0 and (b * s) % ROW_TILE == 0 and s // MOBA_BLOCK <= 31

    w_in0 = w_in[0]
    q0, k0, v0 = POOL_WIDTH, POOL_WIDTH + ATTN_WIDTH, POOL_WIDTH + 2 * ATTN_WIDTH
    w_uk = jnp.concatenate([w_in0[:, :q0], w_in0[:, k0:v0]], axis=1).astype(_BF16)
    w_qv_t = jnp.concatenate([w_in0[:, q0:k0], w_in0[:, v0:]], axis=1).T.astype(_BF16)

    bias_tiles = _bias_tiles(rel_bias)
    ypool, k, qT, vT, sel = _in_proj(x, g_mix_pre, w_uk, w_qv_t, w_pool[0].astype(_BF16),
                                     pool_scale, g_pool_out)
    sel = sel.reshape(b, N_HEADS, s // MOBA_BLOCK, MOBA_BLOCK)
    attn = _attention(rel_bias, qT, k, vT, sel, bias_tiles)

    w_out0 = w_out[0].astype(_BF16)
    out = _out_mlp(x.reshape(b * s, d), ypool.reshape(b * s, POOL_WIDTH),
                   attn.reshape(b * s, ATTN_WIDTH), g_attn_out,
                   w_out0[:POOL_WIDTH], w_out0[POOL_WIDTH:], g_mix_post, g_mlp_pre,
                   w_up[0].astype(_BF16), w_down[0].astype(_BF16), g_mlp_post)
    return out.reshape(b, s, d)
```
